```python
import jax, jax.numpy as jnp
from jax import lax
import numpy as np

D_MODEL = 2048
BATCH = 2
SEQ = 4096
DEPTH = 1
DEC_BATCH = 8
DEC_SEQ = 4
PAST_LEN = 16384
PAGE_SIZE = 128

MIX_WIDTH = D_MODEL
HEAD_DIM = 64
ATTN_WIDTH = MIX_WIDTH // 2
N_HEADS = ATTN_WIDTH // HEAD_DIM
CONV_CH = MIX_WIDTH - ATTN_WIDTH
CONV_WIDTH = 3
IN_WIDTH = 3 * ATTN_WIDTH + 3 * CONV_CH
N_EXPERTS = 32
TOP_K = 4
D_FF = D_MODEL
SWIGLU_ALPHA = 1.702
SWIGLU_LIMIT = 7.0
SB_BIAS_INIT = -8.0
Q_BLOCK = 128
MOE_BLOCK = 128
EPS = 1e-6

kernel_name = 'hymba_conv_stickbreaking_moe_step'


def rmsnorm(x, w):
    x32 = x.astype(jnp.float32)
    y = x32 * lax.rsqrt(jnp.mean(x32 * x32, axis=-1, keepdims=True) + EPS)
    return (y * w.astype(jnp.float32)).astype(x.dtype)


def split_projection(xn, w_in, q_norm_w, k_norm_w):
    b, t, _ = xn.shape
    p = xn @ w_in
    q, k, v, bg, cg, xc = jnp.split(
        p, [ATTN_WIDTH, 2 * ATTN_WIDTH, 3 * ATTN_WIDTH,
            3 * ATTN_WIDTH + CONV_CH, 3 * ATTN_WIDTH + 2 * CONV_CH], axis=-1)
    heads = lambda a: a.reshape(b, t, N_HEADS, HEAD_DIM)
    q = rmsnorm(heads(q), q_norm_w)
    k = rmsnorm(heads(k), k_norm_w)
    return q, k, heads(v), bg, cg, xc


def stick_breaking_attend(q, k, v, q_pos, k_pos, sb_bias):
    z = jnp.einsum('bqhd,bkhd->bhqk', q.astype(jnp.float32), k.astype(jnp.float32)) * (HEAD_DIM ** -0.5)
    z = z + sb_bias.astype(jnp.float32)[None, :, None, None]
    mask = k_pos[None, :] < q_pos[:, None]
    log_keep = jnp.where(mask, jax.nn.log_sigmoid(-z), 0.0)
    later = lax.cumsum(log_keep, axis=3, reverse=True) - log_keep
    a = jnp.where(mask, jnp.exp(jax.nn.log_sigmoid(z) + later), 0.0)
    return jnp.einsum('bhqk,bkhd->bqhd', a, v.astype(jnp.float32)).astype(v.dtype)


def stick_breaking_prompt(q, k, v, sb_bias):
    b, t, h, d = q.shape
    nb = t // Q_BLOCK
    qb = q.reshape(b, nb, Q_BLOCK, h, d).transpose(1, 0, 2, 3, 4)
    pos = jnp.arange(t, dtype=jnp.int32)
    pb = pos.reshape(nb, Q_BLOCK)
    out = lax.map(lambda args: stick_breaking_attend(args[0], k, v, args[1], pos, sb_bias), (qb, pb))
    return out.transpose(1, 0, 2, 3, 4).reshape(b, t, h, d)


def short_conv(bg, cg, xc, buf, conv_w):
    u = cg * xc
    u_pad = jnp.concatenate([buf.astype(u.dtype), u], axis=1)
    t = u.shape[1]
    y = sum(conv_w[j] * u_pad[:, j:j + t] for j in range(CONV_WIDTH))
    return bg * y, u_pad[:, t:]


def merge_groups(x, attn, conv, attn_out_norm_w, conv_out_norm_w, w_out):
    b, t, _ = x.shape
    mixed = jnp.concatenate([rmsnorm(attn.reshape(b, t, ATTN_WIDTH), attn_out_norm_w),
                             rmsnorm(conv, conv_out_norm_w)], axis=-1)
    return x + mixed @ w_out


def clamped_swiglu(h):
    x_glu, x_lin = h[..., :D_FF], h[..., D_FF:]
    x_glu = jnp.minimum(x_glu, SWIGLU_LIMIT)
    x_lin = jnp.clip(x_lin, -SWIGLU_LIMIT, SWIGLU_LIMIT)
    return x_glu * jax.nn.sigmoid(SWIGLU_ALPHA * x_glu) * (x_lin + 1.0)


def moe(xn, w_router, b_router, w1, b1, w2, b2):
    n_tok, d = xn.shape
    logits = (xn @ w_router + b_router).astype(jnp.float32)
    top_logit, top_idx = lax.top_k(logits, TOP_K)
    gates = jax.nn.softmax(top_logit, axis=-1)
    n_assign = n_tok * TOP_K
    flat_e = top_idx.reshape(-1)
    order = jnp.argsort(flat_e)
    sorted_e = flat_e[order]
    counts = jnp.bincount(flat_e, length=N_EXPERTS)
    padded = (counts + MOE_BLOCK - 1) // MOE_BLOCK * MOE_BLOCK
    pad_end = jnp.cumsum(padded)
    dest = (pad_end - padded)[sorted_e] + jnp.arange(n_assign) - (jnp.cumsum(counts) - counts)[sorted_e]
    n_blocks = -(-(n_assign + N_EXPERTS * (MOE_BLOCK - 1)) // MOE_BLOCK)
    n_rows = n_blocks * MOE_BLOCK
    row_tok = jnp.zeros((n_rows,), jnp.int32).at[dest].set((order // TOP_K).astype(jnp.int32))
    row_gate = jnp.zeros((n_rows,), jnp.float32).at[dest].set(gates.reshape(-1)[order])
    block_e = jnp.minimum(jnp.searchsorted(pad_end, jnp.arange(n_blocks) * MOE_BLOCK, side='right'),
                          N_EXPERTS - 1)
    xb = xn[row_tok].reshape(n_blocks, MOE_BLOCK, d)

    def expert_block(args):
        rows, e = args
        return clamped_swiglu(rows @ w1[e] + b1[e]) @ w2[e] + b2[e]

    yb = lax.map(expert_block, (xb, block_e)).reshape(n_rows, d)
    y = jax.ops.segment_sum(yb.astype(jnp.float32) * row_gate[:, None], row_tok, num_segments=n_tok)
    return y.astype(xn.dtype)


def setup_inputs(seed: int = 0) -> dict:
    key = jax.random.key(seed)
    ks = jax.random.split(key, 24)
    n_pages = PAST_LEN // PAGE_SIZE
    n_used = DEC_BATCH * n_pages
    n_phys = n_used + n_used // 4
    nrm = lambda k, shape, s: jax.random.normal(k, shape, jnp.float32) * s
    gain = lambda k, shape: 1.0 + 0.05 * jax.random.normal(k, shape, jnp.float32)
    page_table = jax.random.permutation(ks[5], n_phys)[:n_used].reshape(DEC_BATCH, n_pages).astype(jnp.int32)
    return {
        'x_prompt': nrm(ks[0], (BATCH, SEQ, D_MODEL), 1.0),
        'x_sample': nrm(ks[1], (DEC_BATCH, DEC_SEQ, D_MODEL), 1.0),
        'cache_k': nrm(ks[2], (DEPTH, n_phys, PAGE_SIZE, N_HEADS, HEAD_DIM), 1.0),
        'cache_v': nrm(ks[3], (DEPTH, n_phys, PAGE_SIZE, N_HEADS, HEAD_DIM), 1.0),
        'state_conv': nrm(ks[4], (DEPTH, DEC_BATCH, CONV_WIDTH - 1, CONV_CH), 1.0),
        'page_table': page_table,
        'attn_norm_w': gain(ks[6], (DEPTH, D_MODEL)),
        'w_in': nrm(ks[7], (DEPTH, D_MODEL, IN_WIDTH), D_MODEL ** -0.5),
        'q_norm_w': gain(ks[8], (DEPTH, HEAD_DIM)),
        'k_norm_w': gain(ks[9], (DEPTH, HEAD_DIM)),
        'sb_bias': SB_BIAS_INIT + nrm(ks[21], (DEPTH, N_HEADS), 0.1),
        'conv_w': nrm(ks[10], (DEPTH, CONV_WIDTH, CONV_CH), 0.5),
        'attn_out_norm_w': gain(ks[11], (DEPTH, ATTN_WIDTH)),
        'conv_out_norm_w': gain(ks[12], (DEPTH, CONV_CH)),
        'w_out': nrm(ks[13], (DEPTH, MIX_WIDTH, D_MODEL), MIX_WIDTH ** -0.5),
        'ffn_norm_w': gain(ks[14], (DEPTH, D_MODEL)),
        'w_router': nrm(ks[15], (DEPTH, D_MODEL, N_EXPERTS), D_MODEL ** -0.5),
        'b_router': nrm(ks[16], (DEPTH, N_EXPERTS), 0.01),
        'w1': nrm(ks[17], (DEPTH, N_EXPERTS, D_MODEL, 2 * D_FF), D_MODEL ** -0.5),
        'b1': nrm(ks[18], (DEPTH, N_EXPERTS, 2 * D_FF), 0.01),
        'w2': nrm(ks[19], (DEPTH, N_EXPERTS, D_FF, D_MODEL), D_FF ** -0.5),
        'b2': nrm(ks[20], (DEPTH, N_EXPERTS, D_MODEL), 0.01),
    }


def reference(x_prompt, x_sample, cache_k, cache_v, state_conv, page_table,
              attn_norm_w, w_in, q_norm_w, k_norm_w, sb_bias, conv_w, attn_out_norm_w, conv_out_norm_w,
              w_out, ffn_norm_w, w_router, b_router, w1, b1, w2, b2):
    dec_b, n_pages = page_table.shape
    past_len = n_pages * PAGE_SIZE
    dec_seq = x_sample.shape[1]
    q_pos_s = past_len + jnp.arange(dec_seq, dtype=jnp.int32)
    k_pos_s = jnp.arange(past_len + dec_seq, dtype=jnp.int32)
    n_prompt_tok = x_prompt.shape[0] * x_prompt.shape[1]
    xp, xs = x_prompt, x_sample
    kp_l, vp_l, cp_l, ks_l, vs_l, cs_l = [], [], [], [], [], []
    for l in range(DEPTH):
        q, k, v, bg, cg, xc = split_projection(rmsnorm(xp, attn_norm_w[l]), w_in[l], q_norm_w[l], k_norm_w[l])
        attn = stick_breaking_prompt(q, k, v, sb_bias[l])
        zero_buf = jnp.zeros((xp.shape[0], CONV_WIDTH - 1, CONV_CH), xp.dtype)
        conv, conv_state_p = short_conv(bg, cg, xc, zero_buf, conv_w[l])
        hp = merge_groups(xp, attn, conv, attn_out_norm_w[l], conv_out_norm_w[l], w_out[l])
        kp_l.append(k); vp_l.append(v); cp_l.append(conv_state_p)
        qs, kn, vn, bgs, cgs, xcs = split_projection(rmsnorm(xs, attn_norm_w[l]), w_in[l], q_norm_w[l], k_norm_w[l])
        k_past = cache_k[l][page_table].reshape(dec_b, past_len, N_HEADS, HEAD_DIM)
        v_past = cache_v[l][page_table].reshape(dec_b, past_len, N_HEADS, HEAD_DIM)
        k_all = jnp.concatenate([k_past.astype(kn.dtype), kn], axis=1)
        v_all = jnp.concatenate([v_past.astype(vn.dtype), vn], axis=1)
        attn_s = stick_breaking_attend(qs, k_all, v_all, q_pos_s, k_pos_s, sb_bias[l])
        conv_s, conv_state_s = short_conv(bgs, cgs, xcs, state_conv[l], conv_w[l])
        hs = merge_groups(xs, attn_s, conv_s, attn_out_norm_w[l], conv_out_norm_w[l], w_out[l])
        ks_l.append(kn); vs_l.append(vn); cs_l.append(conv_state_s)
        h = jnp.concatenate([hp.reshape(-1, D_MODEL), hs.reshape(-1, D_MODEL)], axis=0)
        h = h + moe(rmsnorm(h, ffn_norm_w[l]), w_router[l], b_router[l], w1[l], b1[l], w2[l], b2[l])
        xp = h[:n_prompt_tok].reshape(hp.shape)
        xs = h[n_prompt_tok:].reshape(hs.shape)
    new_k_prompt = jnp.stack(kp_l)
    new_v_prompt = jnp.stack(vp_l)
    new_conv_prompt = jnp.stack(cp_l)
    new_k_sample = jnp.stack(ks_l)
    new_v_sample = jnp.stack(vs_l)
    new_conv_sample = jnp.stack(cs_l)
    return (xp, xs, new_k_prompt, new_v_prompt, new_conv_prompt, new_k_sample, new_v_sample, new_conv_sample)
```

```python
import functools

import jax
import jax.numpy as jnp
from jax import lax
from jax.experimental import pallas as pl
from jax.experimental.pallas import tpu as pltpu

F32 = jnp.float32
BF16 = jnp.bfloat16

HEAD_DIM = 64
CONV_WIDTH = 3
TOP_K = 4
SWIGLU_ALPHA = 1.702
SWIGLU_LIMIT = 7.0
EPS = 1e-6

LANES_V7X = 128
MXU_EDGE_V7X = 256
VMEM_LIMIT_V7X = 56 * 1024 * 1024

PROJ_ROWS = 256
ATTN_BLOCK = 256
ROUTE_ROWS = 256
MOE_ROWS = 256
MOE_SUPER_BLOCKS = 8
MOE_FF_CHUNK = 256
COMBINE_ROWS = 128


def _cparams(sem):
    return pltpu.CompilerParams(dimension_semantics=sem, vmem_limit_bytes=VMEM_LIMIT_V7X)


def _rms(x, w):
    ms = jnp.mean(x * x, axis=-1, keepdims=True)
    return x * lax.rsqrt(ms + EPS) * w


def _dot(a, b):
    return jnp.dot(a, b, preferred_element_type=F32)


def _dot_nt(a, b):
    return lax.dot_general(a, b, (((1,), (1,)), ((), ())), preferred_element_type=F32)


def _dot_tn(a, b):
    return lax.dot_general(a, b, (((0,), (0,)), ((), ())), preferred_element_type=F32)


def _softplus(z):
    return jnp.maximum(z, 0.0) + jnp.log(1.0 + jnp.exp(-jnp.abs(z)))


def _qkv_kernel(x_ref, nw_ref, w_ref, qg_ref, kg_ref, gm_ref,
                qb_ref, kf_ref, vf_ref, kb_ref, vb_ref, *, width):
    xn = _rms(x_ref[...], nw_ref[...]).astype(BF16)
    p = _dot(xn, w_ref[...])
    gm = gm_ref[...]

    def head_norm(t, gain):
        sq = t * t
        hi = sq.astype(BF16)
        lo = (sq - hi.astype(F32)).astype(BF16)
        cols = []
        edge = gm.shape[0]
        for c in range(width // edge):
            sl = slice(c * edge, (c + 1) * edge)
            cols.append(_dot(hi[:, sl], gm) + _dot(lo[:, sl], gm))
        ms = jnp.concatenate(cols, axis=1)
        return t * lax.rsqrt(ms + EPS) * gain

    q = head_norm(p[:, :width], qg_ref[...]) * (HEAD_DIM ** -0.5)
    k = head_norm(p[:, width:2 * width], kg_ref[...])
    v = p[:, 2 * width:]
    qb_ref[...] = q.astype(BF16)
    kf_ref[...] = k
    vf_ref[...] = v
    kb_ref[...] = k.astype(BF16)
    vb_ref[...] = v.astype(BF16)


def _qkv_proj(x, norm_w, w_qkv, q_gain, k_gain, head_mean, *, tm):
    rows, d = x.shape
    width = w_qkv.shape[1] // 3
    const = lambda i: (0, 0)
    row = lambda i: (i, 0)
    out_f = jax.ShapeDtypeStruct((rows, width), F32)
    out_b = jax.ShapeDtypeStruct((rows, width), BF16)
    return pl.pallas_call(
        functools.partial(_qkv_kernel, width=width),
        out_shape=(out_b, out_f, out_f, out_b, out_b),
        grid=(rows // tm,),
        in_specs=[pl.BlockSpec((tm, d), row),
                  pl.BlockSpec((1, d), const),
                  pl.BlockSpec(w_qkv.shape, const),
                  pl.BlockSpec((1, width), const),
                  pl.BlockSpec((1, width), const),
                  pl.BlockSpec(head_mean.shape, const)],
        out_specs=tuple(pl.BlockSpec((tm, width), row) for _ in range(5)),
        compiler_params=_cparams(("arbitrary",)),
        name="qkv_proj",
    )(x, norm_w, w_qkv, q_gain, k_gain, head_mean)


def _conv_body(x_ref, nw_ref, w_ref, cw_ref, og_ref, ch):
    xn = _rms(x_ref[...], nw_ref[...]).astype(BF16)
    p = _dot(xn, w_ref[...])
    return p[:, :ch], p[:, ch:2 * ch] * p[:, 2 * ch:]


def _conv_finish(bg, u, u1, u2, cw_ref, og_ref, out_ref):
    cw = cw_ref[...]
    y = bg * (cw[0:1, :] * u2 + cw[1:2, :] * u1 + cw[2:3, :] * u)
    out_ref[...] = _rms(y, og_ref[...]).astype(BF16)


def _conv_prompt_kernel(x_ref, nw_ref, w_ref, cw_ref, og_ref, out_ref, tail_ref, carry_ref,
                        *, ch, tiles_per_seq):
    bg, u = _conv_body(x_ref, nw_ref, w_ref, cw_ref, og_ref, ch)
    tm = u.shape[0]
    first = (pl.program_id(0) % tiles_per_seq) == 0
    prev = jnp.where(first, 0.0, carry_ref[...])
    c1, c2 = prev[7:8, :], prev[6:7, :]
    r = lax.broadcasted_iota(jnp.int32, u.shape, 0)
    u1 = jnp.where(r == 0, c1, pltpu.roll(u, 1, 0))
    u2 = jnp.where(r == 0, c2, jnp.where(r == 1, c1, pltpu.roll(u, 2, 0)))
    tail = u[tm - 8:, :]
    carry_ref[...] = tail
    tail_ref[0] = tail
    _conv_finish(bg, u, u1, u2, cw_ref, og_ref, out_ref)


def _conv_sample_kernel(x_ref, nw_ref, w_ref, cw_ref, og_ref, f1_ref, f2_ref, out_ref, u_ref,
                        *, ch, seq):
    bg, u = _conv_body(x_ref, nw_ref, w_ref, cw_ref, og_ref, ch)
    pos = lax.rem(lax.broadcasted_iota(jnp.int32, u.shape, 0), seq)
    u1 = jnp.where(pos == 0, f1_ref[...], pltpu.roll(u, 1, 0))
    u2 = jnp.where(pos < 2, f2_ref[...], pltpu.roll(u, 2, 0))
    u_ref[...] = u
    _conv_finish(bg, u, u1, u2, cw_ref, og_ref, out_ref)


def _conv_prompt(x, norm_w, w_conv, conv_w, out_gain, *, tm, seq):
    rows, d = x.shape
    ch = w_conv.shape[1] // 3
    n_tiles = rows // tm
    const = lambda i: (0, 0)
    row = lambda i: (i, 0)
    return pl.pallas_call(
        functools.partial(_conv_prompt_kernel, ch=ch, tiles_per_seq=seq // tm),
        out_shape=(jax.ShapeDtypeStruct((rows, ch), BF16),
                   jax.ShapeDtypeStruct((n_tiles, 8, ch), F32)),
        grid=(n_tiles,),
        in_specs=[pl.BlockSpec((tm, d), row),
                  pl.BlockSpec((1, d), const),
                  pl.BlockSpec(w_conv.shape, const),
                  pl.BlockSpec(conv_w.shape, const),
                  pl.BlockSpec((1, ch), const)],
        out_specs=(pl.BlockSpec((tm, ch), row),
                   pl.BlockSpec((1, 8, ch), lambda i: (i, 0, 0))),
        scratch_shapes=[pltpu.VMEM((8, ch), F32)],
        compiler_params=_cparams(("arbitrary",)),
        name="conv_prompt",
    )(x, norm_w, w_conv, conv_w, out_gain)


def _conv_sample(x, norm_w, w_conv, conv_w, out_gain, fix1, fix2, *, seq):
    rows, d = x.shape
    ch = w_conv.shape[1] // 3
    full = lambda a: pl.BlockSpec(a.shape, lambda i: (0, 0))
    return pl.pallas_call(
        functools.partial(_conv_sample_kernel, ch=ch, seq=seq),
        out_shape=(jax.ShapeDtypeStruct((rows, ch), BF16),
                   jax.ShapeDtypeStruct((rows, ch), F32)),
        grid=(1,),
        in_specs=[full(x), full(norm_w), full(w_conv), full(conv_w), full(out_gain),
                  full(fix1), full(fix2)],
        out_specs=(pl.BlockSpec((rows, ch), lambda i: (0, 0)),
                   pl.BlockSpec((rows, ch), lambda i: (0, 0))),
        compiler_params=_cparams(("arbitrary",)),
        name="conv_sample",
    )(x, norm_w, w_conv, conv_w, out_gain, fix1, fix2)


def _sb_block(z, upper, carry, valid):
    sp = _softplus(z)
    lsig = z - sp
    if valid is not None:
        sp = jnp.where(valid, sp, 0.0)
    later = _dot(upper, sp.astype(BF16))
    a = jnp.exp(lsig - later - carry)
    if valid is not None:
        a = jnp.where(valid, a, 0.0)
    return a, carry + later[0:1, :] + sp[0:1, :]


def _attn_prompt_kernel(bias_ref, q_ref, k_ref, v_ref, up_ref, o_ref, *, blk):
    hp = pl.program_id(1)
    qi = pl.program_id(2)
    q2 = q_ref[0]
    lane = lax.broadcasted_iota(jnp.int32, (1, 2 * HEAD_DIM), 1)
    head_lanes = (lane < HEAD_DIM, lane >= HEAD_DIM)
    zero = jnp.zeros((), BF16)
    qm = tuple(jnp.where(m, q2, zero) for m in head_lanes)
    bias = (bias_ref[2 * hp], bias_ref[2 * hp + 1])
    upper = up_ref[...]

    def block(kb, carries, acc, valid):
        start = pl.multiple_of(kb * blk, blk)
        k2 = k_ref[0, pl.ds(start, blk), :]
        v2 = v_ref[0, pl.ds(start, blk), :]
        new_carries = []
        for h in range(2):
            z = _dot_nt(k2, qm[h]) + bias[h]
            a, c = _sb_block(z, upper, carries[h], valid)
            vm = jnp.where(head_lanes[h], v2, zero)
            acc = acc + _dot_tn(vm, a.astype(BF16))
            new_carries.append(c)
        return tuple(new_carries), acc

    s_idx = lax.broadcasted_iota(jnp.int32, (blk, blk), 0)
    t_idx = lax.broadcasted_iota(jnp.int32, (blk, blk), 1)
    zero_row = jnp.zeros((1, blk), F32)
    carries, acc = block(qi, (zero_row, zero_row), jnp.zeros((2 * HEAD_DIM, blk), F32),
                         s_idx < t_idx)

    def body(it, state):
        return block(qi - 1 - it, state[0], state[1], None)

    carries, acc = lax.fori_loop(0, qi, body, (carries, acc))
    o_ref[0] = acc.T


def _attn_prompt(qb, kb, vb, sb_bias, upper, *, blk):
    b, t, width = qb.shape
    n_pairs = width // (2 * HEAD_DIM)
    grid_spec = pltpu.PrefetchScalarGridSpec(
        num_scalar_prefetch=1,
        grid=(b, n_pairs, t // blk),
        in_specs=[pl.BlockSpec((1, blk, 2 * HEAD_DIM), lambda bi, hp, qi, s: (bi, qi, hp)),
                  pl.BlockSpec((1, t, 2 * HEAD_DIM), lambda bi, hp, qi, s: (bi, 0, hp)),
                  pl.BlockSpec((1, t, 2 * HEAD_DIM), lambda bi, hp, qi, s: (bi, 0, hp)),
                  pl.BlockSpec((blk, blk), lambda bi, hp, qi, s: (0, 0))],
        out_specs=pl.BlockSpec((1, blk, 2 * HEAD_DIM), lambda bi, hp, qi, s: (bi, qi, hp)),
    )
    return pl.pallas_call(
        functools.partial(_attn_prompt_kernel, blk=blk),
        out_shape=jax.ShapeDtypeStruct((b, t, width), F32),
        grid_spec=grid_spec,
        compiler_params=_cparams(("arbitrary", "arbitrary", "arbitrary")),
        name="attn_prompt",
    )(sb_bias, qb, kb, vb, upper)


def _attn_sample_kernel(pt_ref, qbd_ref, bias_ref, kn_ref, vn_ref, kp_ref, vp_ref,
                        up_new_ref, up_ref, o_ref, acc_ref, carry_ref, *, n_heads, n_new):
    p = pl.program_id(1)
    qbd = qbd_ref[0]
    bias = bias_ref[...]

    @pl.when(p == 0)
    def _new_tokens():
        kn = kn_ref[0].astype(BF16)
        z = _dot(kn, qbd) + bias
        j = lax.broadcasted_iota(jnp.int32, z.shape, 0)
        i = lax.broadcasted_iota(jnp.int32, z.shape, 1) // n_heads
        a, c = _sb_block(z, up_new_ref[...], jnp.zeros_like(bias), j < i)
        acc_ref[...] = _dot_tn(a.astype(BF16), vn_ref[0].astype(BF16))
        carry_ref[...] = c

    z = _dot(kp_ref[0].astype(BF16), qbd) + bias
    a, c = _sb_block(z, up_ref[...], carry_ref[...], None)
    acc_ref[...] += _dot_tn(a.astype(BF16), vp_ref[0].astype(BF16))
    carry_ref[...] = c

    @pl.when(p == pl.num_programs(1) - 1)
    def _finish():
        acc = acc_ref[...]
        h_col = lax.broadcasted_iota(jnp.int32, (n_heads, acc.shape[1]), 1) // HEAD_DIM
        h_row = lax.broadcasted_iota(jnp.int32, (n_heads, acc.shape[1]), 0)
        own = h_col == h_row
        rows = [jnp.sum(jnp.where(own, acc[i * n_heads:(i + 1) * n_heads, :], 0.0),
                        axis=0, keepdims=True) for i in range(n_new)]
        o_ref[0] = jnp.concatenate(rows, axis=0)


def _attn_sample(page_table, qbd, bias_cols, k_new, v_new, cache_k, cache_v, up_new, upper,
                 *, n_heads, n_new):
    n_seq, n_pages = page_table.shape
    _, page, width = cache_k.shape
    cols = n_new * n_heads

    def page_map(b, p, pt):
        return (pt[b * n_pages + n_pages - 1 - p], 0, 0)

    per_seq = lambda b, p, pt: (b, 0, 0)
    const = lambda b, p, pt: (0, 0)
    grid_spec = pltpu.PrefetchScalarGridSpec(
        num_scalar_prefetch=1,
        grid=(n_seq, n_pages),
        in_specs=[pl.BlockSpec((1, width, cols), per_seq),
                  pl.BlockSpec((1, cols), const),
                  pl.BlockSpec((1,) + k_new.shape[1:], per_seq),
                  pl.BlockSpec((1,) + v_new.shape[1:], per_seq),
                  pl.BlockSpec((1, page, width), page_map),
                  pl.BlockSpec((1, page, width), page_map),
                  pl.BlockSpec(up_new.shape, const),
                  pl.BlockSpec(upper.shape, const)],
        out_specs=pl.BlockSpec((1, n_new, width), per_seq),
        scratch_shapes=[pltpu.VMEM((cols, width), F32), pltpu.VMEM((1, cols), F32)],
    )
    return pl.pallas_call(
        functools.partial(_attn_sample_kernel, n_heads=n_heads, n_new=n_new),
        out_shape=jax.ShapeDtypeStruct((n_seq, n_new, width), F32),
        grid_spec=grid_spec,
        compiler_params=_cparams(("arbitrary", "arbitrary")),
        name="attn_sample",
    )(page_table.reshape(-1), qbd, bias_cols, k_new, v_new, cache_k, cache_v, up_new, upper)


def _route_kernel(attn_ref, conv_ref, x_ref, ag_ref, wo_ref, fg_ref, wr_ref, br_ref, cin_ref,
                  ut_ref, ones_ref, *rest, n_valid, tile0, n_exp):
    h_ref, xp_ref, topi_ref, gate_ref, rank_ref, cnt_ref, cnt_scr = rest[-7:]
    i = pl.program_id(0)
    tm = x_ref.shape[0]
    half = attn_ref.shape[1]

    @pl.when(i == 0)
    def _():
        cnt_scr[...] = cin_ref[...]

    an = _rms(attn_ref[...], ag_ref[...]).astype(BF16)
    mix = _dot(an, wo_ref[:half, :]) + _dot(conv_ref[...], wo_ref[half:, :])
    h = x_ref[...] + mix
    h_ref[...] = h

    xn = _rms(h, fg_ref[...])
    hi = xn.astype(BF16)
    hi32 = hi.astype(F32)
    lo = (xn - hi32).astype(BF16)
    d2 = xn.shape[1] // 2
    bits = pltpu.bitcast(hi32, jnp.uint32)
    xp_ref[...] = (bits[:, :d2] & jnp.uint32(0xFFFF0000)) | (bits[:, d2:] >> 16)

    wr = wr_ref[...]
    l1 = _dot_nt(wr, hi)
    l2 = _dot_nt(wr[:n_exp, :], lo)
    logits = l1[:n_exp, :] + l1[n_exp:, :] + l2 + br_ref[...]

    e_idx = lax.broadcasted_iota(jnp.int32, logits.shape, 0)
    rest = logits
    tops, idxs, hots = [], [], []
    for _ in range(TOP_K):
        m = jnp.max(rest, axis=0, keepdims=True)
        idx = jnp.min(jnp.where(rest == m, e_idx, n_exp), axis=0, keepdims=True)
        hot = e_idx == idx
        rest = jnp.where(hot, -jnp.inf, rest)
        tops.append(m)
        idxs.append(idx)
        hots.append(hot)
    ex = [jnp.exp(m - tops[0]) for m in tops]
    denom = ex[0] + ex[1] + ex[2] + ex[3]
    gates = [e / denom for e in ex]

    tok = (tile0 + i) * tm + lax.broadcasted_iota(jnp.int32, (1, tm), 1)
    valid = tok < n_valid
    chosen = jnp.where((hots[0] | hots[1] | hots[2] | hots[3]) & valid, 1.0, 0.0).astype(BF16)
    before = _dot(chosen, ut_ref[...])
    cnt = cnt_scr[...]
    base = jnp.concatenate([cnt] * (tm // LANES_V7X), axis=1) + before
    ranks = [jnp.sum(jnp.where(hot, base, 0.0), axis=0, keepdims=True) for hot in hots]
    cnt = cnt + _dot(chosen, ones_ref[...])
    cnt_scr[...] = cnt
    cnt_ref[...] = cnt

    topi_ref[...] = jnp.concatenate(idxs, axis=0)
    gate_ref[...] = jnp.concatenate(gates, axis=0)
    rank_ref[...] = jnp.concatenate(ranks, axis=0).astype(jnp.int32)


def _route(attn, convn, x, attn_gain, w_out, ffn_gain, wr_t, br_cols, cnt_in, ut, ones,
           prev, *, tm, tile0, n_valid, rows_total):
    rows, d = x.shape
    half = attn.shape[1]
    n_exp = br_cols.shape[0]
    n_tiles = rows // tm
    const = lambda i: (0, 0)
    row = lambda i: (i, 0)
    orow = lambda i: (tile0 + i, 0)
    ocol = lambda i: (0, tile0 + i)
    any_spec = pl.BlockSpec(memory_space=pl.ANY)
    out_shape = (jax.ShapeDtypeStruct((rows_total, d), F32),
                 jax.ShapeDtypeStruct((rows_total, d // 2), jnp.uint32),
                 jax.ShapeDtypeStruct((TOP_K, rows_total), jnp.int32),
                 jax.ShapeDtypeStruct((TOP_K, rows_total), F32),
                 jax.ShapeDtypeStruct((TOP_K, rows_total), jnp.int32),
                 jax.ShapeDtypeStruct((n_exp, LANES_V7X), F32))
    prev = () if prev is None else tuple(prev)
    aliases = {11 + j: j for j in range(len(prev))}
    return pl.pallas_call(
        functools.partial(_route_kernel, n_valid=n_valid, tile0=tile0, n_exp=n_exp),
        out_shape=out_shape,
        grid=(n_tiles,),
        in_specs=[pl.BlockSpec((tm, half), row),
                  pl.BlockSpec((tm, half), row),
                  pl.BlockSpec((tm, d), row),
                  pl.BlockSpec((1, half), const),
                  pl.BlockSpec(w_out.shape, const),
                  pl.BlockSpec((1, d), const),
                  pl.BlockSpec(wr_t.shape, const),
                  pl.BlockSpec(br_cols.shape, const),
                  pl.BlockSpec(cnt_in.shape, const),
                  pl.BlockSpec(ut.shape, const),
                  pl.BlockSpec(ones.shape, const)] + [any_spec] * len(prev),
        out_specs=(pl.BlockSpec((tm, d), orow),
                   pl.BlockSpec((tm, d // 2), orow),
                   pl.BlockSpec((TOP_K, tm), ocol),
                   pl.BlockSpec((TOP_K, tm), ocol),
                   pl.BlockSpec((TOP_K, tm), ocol),
                   pl.BlockSpec((n_exp, LANES_V7X), const)),
        scratch_shapes=[pltpu.VMEM((n_exp, LANES_V7X), F32)],
        input_output_aliases=aliases,
        compiler_params=_cparams(("arbitrary",)),
        name="route",
    )(attn, convn, x, attn_gain, w_out, ffn_gain, wr_t, br_cols, cnt_in, ut, ones, *prev)


def _moe_kernel(sbe_ref, sb0_ref, sbn_ref, tok_hbm, x_hbm, w1g_ref, w1l_ref, b1g_ref, b1l_ref,
                w2_ref, b2_ref, y_hbm, xbuf, yacc, w1g_bf, w1l_bf, w2_bf, tok_smem,
                sem_tok, sem_x, sem_y, *, tmb):
    g = pl.program_id(0)
    fc = pl.program_id(1)
    n_fc = pl.num_programs(1)
    nb = sbn_ref[g]
    blk0 = sb0_ref[g]
    d2 = xbuf.shape[1]

    @pl.when(nb > 0)
    def _active():
        @pl.when(fc == 0)
        def _gather():
            def per_block(rb, carry):
                cp = pltpu.make_async_copy(tok_hbm.at[blk0 + rb], tok_smem, sem_tok)
                cp.start()
                cp.wait()
                base = pl.multiple_of(rb * tmb, tmb)

                def per_row(r, c):
                    t = tok_smem[r]
                    pltpu.make_async_copy(x_hbm.at[pl.ds(t, 1), :],
                                          xbuf.at[pl.ds(base + r, 1), :], sem_x).start()
                    return c

                lax.fori_loop(0, tmb, per_row, 0)
                pltpu.make_async_copy(x_hbm.at[pl.ds(0, tmb), :],
                                      xbuf.at[pl.ds(base, tmb), :], sem_x).wait()
                return carry

            lax.fori_loop(0, nb, per_block, 0)

        w1g_bf[...] = w1g_ref[0].astype(BF16)
        w1l_bf[...] = w1l_ref[0].astype(BF16)
        w2_bf[...] = w2_ref[0].astype(BF16)

        def per_block(rb, carry):
            base = pl.multiple_of(rb * tmb, tmb)
            xp = xbuf[pl.ds(base, tmb), :]
            xa = pltpu.bitcast(xp & jnp.uint32(0xFFFF0000), F32).astype(BF16)
            xb = pltpu.bitcast(xp << 16, F32).astype(BF16)
            hg = _dot(xa, w1g_bf[:d2, :]) + _dot(xb, w1g_bf[d2:, :]) + b1g_ref[0]
            hl = _dot(xa, w1l_bf[:d2, :]) + _dot(xb, w1l_bf[d2:, :]) + b1l_ref[0]
            hg = jnp.minimum(hg, SWIGLU_LIMIT)
            hl = jnp.clip(hl, -SWIGLU_LIMIT, SWIGLU_LIMIT)
            act = hg * jax.nn.sigmoid(SWIGLU_ALPHA * hg) * (hl + 1.0)
            part = _dot(act.astype(BF16), w2_bf[...])

            @pl.when(fc == 0)
            def _():
                yacc[pl.ds(base, tmb), :] = part + b2_ref[0]

            @pl.when(fc > 0)
            def _():
                yacc[pl.ds(base, tmb), :] += part

            return carry

        lax.fori_loop(0, nb, per_block, 0)

        @pl.when(fc == n_fc - 1)
        def _store():
            def start(rb, carry):
                base = pl.multiple_of(rb * tmb, tmb)
                dst = pl.multiple_of((blk0 + rb) * tmb, tmb)
                pltpu.make_async_copy(yacc.at[pl.ds(base, tmb), :],
                                      y_hbm.at[pl.ds(dst, tmb), :], sem_y).start()
                return carry

            def wait(rb, carry):
                pltpu.make_async_copy(yacc.at[pl.ds(0, tmb), :],
                                      y_hbm.at[pl.ds(0, tmb), :], sem_y).wait()
                return carry

            lax.fori_loop(0, nb, start, 0)
            lax.fori_loop(0, nb, wait, 0)


def _moe(sb_expert, sb_block0, sb_nblocks, row_tok, xp_all, w1, b1, w2, b2, *, n_blocks):
    n_exp, d, f2 = w1.shape
    f = f2 // 2
    tmb = row_tok.shape[1]
    fcw = MOE_FF_CHUNK
    n_fc = f // fcw
    n_super = sb_expert.shape[0]
    sb_rows = MOE_SUPER_BLOCKS * tmb
    any_spec = pl.BlockSpec(memory_space=pl.ANY)
    grid_spec = pltpu.PrefetchScalarGridSpec(
        num_scalar_prefetch=3,
        grid=(n_super, n_fc),
        in_specs=[any_spec, any_spec,
                  pl.BlockSpec((1, d, fcw), lambda g, c, e, b0, nb: (e[g], 0, c)),
                  pl.BlockSpec((1, d, fcw), lambda g, c, e, b0, nb: (e[g], 0, n_fc + c)),
                  pl.BlockSpec((1, 1, fcw), lambda g, c, e, b0, nb: (e[g], 0, c)),
                  pl.BlockSpec((1, 1, fcw), lambda g, c, e, b0, nb: (e[g], 0, n_fc + c)),
                  pl.BlockSpec((1, fcw, d), lambda g, c, e, b0, nb: (e[g], c, 0)),
                  pl.BlockSpec((1, 1, d), lambda g, c, e, b0, nb: (e[g], 0, 0))],
        out_specs=any_spec,
        scratch_shapes=[pltpu.VMEM((sb_rows, d // 2), jnp.uint32),
                        pltpu.VMEM((sb_rows, d), F32),
                        pltpu.VMEM((d, fcw), BF16),
                        pltpu.VMEM((d, fcw), BF16),
                        pltpu.VMEM((fcw, d), BF16),
                        pltpu.SMEM((tmb,), jnp.int32),
                        pltpu.SemaphoreType.DMA,
                        pltpu.SemaphoreType.DMA,
                        pltpu.SemaphoreType.DMA],
    )
    return pl.pallas_call(
        functools.partial(_moe_kernel, tmb=tmb),
        out_shape=jax.ShapeDtypeStruct((n_blocks * tmb, d), F32),
        grid_spec=grid_spec,
        compiler_params=_cparams(("arbitrary", "arbitrary")),
        name="moe_experts",
    )(sb_expert, sb_block0, sb_nblocks, row_tok, xp_all, w1, w1,
      b1.reshape(n_exp, 1, f2), b1.reshape(n_exp, 1, f2), w2, b2.reshape(n_exp, 1, d))


def _combine_kernel(dest_hbm, y_hbm, h_ref, gate_ref, o_ref, ybuf, dest_smem, sem_d, sem_y,
                    *, tile0):
    i = pl.program_id(0)
    tm = h_ref.shape[0]
    cp = pltpu.make_async_copy(dest_hbm.at[tile0 + i], dest_smem, sem_d)
    cp.start()
    cp.wait()

    def per_row(j, c):
        r = dest_smem[j]
        pltpu.make_async_copy(y_hbm.at[pl.ds(r, 1), :], ybuf.at[pl.ds(j, 1), :], sem_y).start()
        return c

    lax.fori_loop(0, TOP_K * tm, per_row, 0)
    pltpu.make_async_copy(y_hbm.at[pl.ds(0, TOP_K * tm), :], ybuf, sem_y).wait()
    out = h_ref[...]
    gates = gate_ref[...]
    for k in range(TOP_K):
        out = out + gates[:, k:k + 1] * ybuf[k * tm:(k + 1) * tm, :]
    o_ref[...] = out


def _combine(dest, y_rows, h_all, gates_t, *, tm, tile0, n_tiles):
    d = h_all.shape[1]
    any_spec = pl.BlockSpec(memory_space=pl.ANY)
    return pl.pallas_call(
        functools.partial(_combine_kernel, tile0=tile0),
        out_shape=jax.ShapeDtypeStruct((n_tiles * tm, d), F32),
        grid=(n_tiles,),
        in_specs=[any_spec, any_spec,
                  pl.BlockSpec((tm, d), lambda i: (tile0 + i, 0)),
                  pl.BlockSpec((tm, TOP_K), lambda i: (tile0 + i, 0))],
        out_specs=pl.BlockSpec((tm, d), lambda i: (i, 0)),
        scratch_shapes=[pltpu.VMEM((TOP_K * tm, d), F32),
                        pltpu.SMEM((TOP_K * tm,), jnp.int32),
                        pltpu.SemaphoreType.DMA,
                        pltpu.SemaphoreType.DMA],
        compiler_params=_cparams(("arbitrary",)),
        name="moe_combine",
    )(dest, y_rows, h_all, gates_t)


def _strict_upper(n, dtype=BF16):
    r = lax.broadcasted_iota(jnp.int32, (n, n), 0)
    c = lax.broadcasted_iota(jnp.int32, (n, n), 1)
    return (r < c).astype(dtype)


def _layer(xp, xs, cache_k, cache_v, state_conv, page_table, attn_norm_w, w_in, q_norm_w,
           k_norm_w, sb_bias, conv_w, attn_out_norm_w, conv_out_norm_w, w_out, ffn_norm_w,
           w_router, b_router, w1, b1, w2, b2):
    n_b, seq, d = xp.shape
    n_dec, n_new, _ = xs.shape
    width = w_in.shape[1] // 6
    n_heads = width // HEAD_DIM
    n_exp = w_router.shape[1]
    n_prompt = n_b * seq
    n_sample = n_dec * n_new
    assert n_new >= CONV_WIDTH - 1 and conv_w.shape[0] == CONV_WIDTH
    assert seq % ATTN_BLOCK == 0 and seq % PROJ_ROWS == 0 and n_prompt % ROUTE_ROWS == 0
    assert n_sample <= COMBINE_ROWS and n_prompt % COMBINE_ROWS == 0

    row = lambda a: a.reshape(1, -1)
    w_in_b = w_in.astype(BF16)
    w_qkv, w_conv = w_in_b[:, :3 * width], w_in_b[:, 3 * width:]
    w_out_b = w_out.astype(BF16)
    edge = min(MXU_EDGE_V7X, width)
    head_in_tile = (lax.broadcasted_iota(jnp.int32, (edge, edge), 0) // HEAD_DIM ==
                    lax.broadcasted_iota(jnp.int32, (edge, edge), 1) // HEAD_DIM)
    head_mean = jnp.where(head_in_tile, 1.0 / HEAD_DIM, 0.0).astype(BF16)
    q_gain = row(jnp.tile(q_norm_w, n_heads))
    k_gain = row(jnp.tile(k_norm_w, n_heads))

    xp2 = xp.reshape(n_prompt, d)
    qb, kf, vf, kb, vb = _qkv_proj(xp2, row(attn_norm_w), w_qkv, q_gain, k_gain, head_mean,
                                   tm=PROJ_ROWS)
    convn_p, tails = _conv_prompt(xp2, row(attn_norm_w), w_conv, conv_w, row(conv_out_norm_w),
                                  tm=PROJ_ROWS, seq=seq)
    tiles_per_seq = seq // PROJ_ROWS
    conv_state_p = tails[tiles_per_seq - 1::tiles_per_seq, 8 - (CONV_WIDTH - 1):, :]
    shp = (n_b, seq, width)
    attn_p = _attn_prompt(qb.reshape(shp), kb.reshape(shp), vb.reshape(shp), sb_bias,
                          _strict_upper(ATTN_BLOCK), blk=ATTN_BLOCK)

    xs2 = xs.reshape(n_sample, d)
    qsb, ksf, vsf, _, _ = _qkv_proj(xs2, row(attn_norm_w), w_qkv, q_gain, k_gain, head_mean,
                                    tm=n_sample)
    zeros_fix = jnp.zeros((n_dec, n_new, width), F32)
    fix1 = zeros_fix.at[:, 0].set(state_conv[:, 1]).reshape(n_sample, width)
    fix2 = zeros_fix.at[:, 0].set(state_conv[:, 0]).at[:, 1].set(state_conv[:, 1])
    convn_s, u_s = _conv_sample(xs2, row(attn_norm_w), w_conv, conv_w, row(conv_out_norm_w),
                                fix1, fix2.reshape(n_sample, width), seq=n_new)
    conv_state_s = u_s.reshape(n_dec, n_new, width)[:, n_new - (CONV_WIDTH - 1):, :]

    q4 = qsb.reshape(n_dec, n_new, n_heads, HEAD_DIM)
    eye = jnp.eye(n_heads, dtype=BF16)
    qbd = jnp.einsum("bihd,hg->bhdig", q4, eye).reshape(n_dec, width, n_new * n_heads)
    bias_cols = row(jnp.tile(sb_bias, n_new))
    pad_new = lambda a: jnp.pad(a.reshape(n_dec, n_new, width), ((0, 0), (0, 16 - n_new), (0, 0)))
    n_phys, page = cache_k.shape[0], cache_k.shape[1]
    attn_s = _attn_sample(page_table, qbd, bias_cols, pad_new(ksf), pad_new(vsf),
                          cache_k.reshape(n_phys, page, width), cache_v.reshape(n_phys, page, width),
                          _strict_upper(16), _strict_upper(page), n_heads=n_heads, n_new=n_new)

    rows_total = n_prompt + COMBINE_ROWS
    n_tok = n_prompt + n_sample
    wr_hi = w_router.astype(BF16)
    wr_lo = (w_router - wr_hi.astype(F32)).astype(BF16)
    wr_t = jnp.concatenate([wr_hi.T, wr_lo.T], axis=0)
    ones = jnp.ones((ROUTE_ROWS, LANES_V7X), BF16)
    ut = _strict_upper(ROUTE_ROWS)
    br = lambda tm: jnp.broadcast_to(b_router[:, None], (n_exp, tm))
    cnt0 = jnp.zeros((n_exp, LANES_V7X), F32)
    outs = _route(attn_p.reshape(n_prompt, width), convn_p, xp2, row(attn_out_norm_w), w_out_b,
                  row(ffn_norm_w), wr_t, br(ROUTE_ROWS), cnt0, ut, ones, None,
                  tm=ROUTE_ROWS, tile0=0, n_valid=n_tok, rows_total=rows_total)
    pad_s = lambda a: jnp.pad(a, ((0, COMBINE_ROWS - n_sample), (0, 0)))
    outs = _route(pad_s(attn_s.reshape(n_sample, width)), pad_s(convn_s), pad_s(xs2),
                  row(attn_out_norm_w), w_out_b, row(ffn_norm_w), wr_t, br(COMBINE_ROWS),
                  outs[5], ut[:COMBINE_ROWS, :COMBINE_ROWS], ones[:COMBINE_ROWS], outs[:5],
                  tm=COMBINE_ROWS, tile0=n_prompt // COMBINE_ROWS, n_valid=n_tok,
                  rows_total=rows_total)
    h_all, xp_all, topi, gates, ranks, cnt = outs

    tmb = MOE_ROWS
    counts = cnt[:, 0].astype(jnp.int32)
    nblk = (counts + tmb - 1) // tmb
    blk_end = jnp.cumsum(nblk)
    blk_off = blk_end - nblk
    n_blocks = (n_tok * TOP_K + n_exp * (tmb - 1)) // tmb
    tok_ids = jnp.arange(rows_total, dtype=jnp.int32)
    dest = jnp.where(tok_ids[None, :] < n_tok, blk_off[topi] * tmb + ranks, n_blocks * tmb)
    row_tok = jnp.zeros((n_blocks * tmb,), jnp.int32).at[dest.reshape(-1)].set(
        jnp.tile(tok_ids, TOP_K), mode="drop").reshape(n_blocks, tmb)
    dest = jnp.where(tok_ids[None, :] < n_tok, dest, 0)

    rb = MOE_SUPER_BLOCKS
    nsb = (nblk + rb - 1) // rb
    sb_end = jnp.cumsum(nsb)
    n_super = n_exp + n_blocks // rb
    gidx = jnp.arange(n_super, dtype=jnp.int32)
    last = jnp.maximum(sb_end[-1] - 1, 0)
    ge = jnp.minimum(gidx, last)
    sb_expert = jnp.searchsorted(sb_end, ge, side="right").astype(jnp.int32)
    local = ge - (sb_end - nsb)[sb_expert]
    sb_block0 = (blk_off[sb_expert] + local * rb).astype(jnp.int32)
    sb_nblocks = jnp.where(gidx < sb_end[-1],
                           jnp.clip(nblk[sb_expert] - local * rb, 0, rb), 0).astype(jnp.int32)

    y_rows = _moe(sb_expert, sb_block0, sb_nblocks, row_tok, xp_all, w1, b1, w2, b2,
                  n_blocks=n_blocks)

    tmc = COMBINE_ROWS
    dest_tiles = dest.T.reshape(rows_total // tmc, tmc, TOP_K).transpose(0, 2, 1).reshape(
        rows_total // tmc, TOP_K * tmc)
    gates_t = gates.T
    yp = _combine(dest_tiles, y_rows, h_all, gates_t, tm=tmc, tile0=0, n_tiles=n_prompt // tmc)
    ys = _combine(dest_tiles, y_rows, h_all, gates_t, tm=tmc, tile0=n_prompt // tmc, n_tiles=1)

    new_shape_p = (n_b, seq, n_heads, HEAD_DIM)
    new_shape_s = (n_dec, n_new, n_heads, HEAD_DIM)
    return (yp.reshape(n_b, seq, d), ys[:n_sample].reshape(n_dec, n_new, d),
            kf.reshape(new_shape_p), vf.reshape(new_shape_p), conv_state_p,
            ksf.reshape(new_shape_s), vsf.reshape(new_shape_s), conv_state_s)


def kernel(x_prompt, x_sample, cache_k, cache_v, state_conv, page_table, attn_norm_w, w_in,
           q_norm_w, k_norm_w, sb_bias, conv_w, attn_out_norm_w, conv_out_norm_w, w_out,
           ffn_norm_w, w_router, b_router, w1, b1, w2, b2):
    depth = w_in.shape[0]
    xp, xs = x_prompt, x_sample
    outs = [[] for _ in range(6)]
    for l in range(depth):
        res = _layer(xp, xs, cache_k[l], cache_v[l], state_conv[l], page_table, attn_norm_w[l],
                     w_in[l], q_norm_w[l], k_norm_w[l], sb_bias[l], conv_w[l],
                     attn_out_norm_w[l], conv_out_norm_w[l], w_out[l], ffn_norm_w[l],
                     w_router[l], b_router[l], w1[l], b1[l], w2[l], b2[l])
        xp, xs = res[0], res[1]
        for acc, val in zip(outs, res[2:]):
            acc.append(val)
    return (xp, xs) + tuple(jnp.stack(o) for o in outs)
```

```python
import functools
import math

import jax
import jax.numpy as jnp
from jax import lax
from jax.experimental import pallas as pl
from jax.experimental.pallas import tpu as pltpu

F32 = jnp.float32
BF16 = jnp.bfloat16

HEAD_DIM = 64
CONV_WIDTH = 3
TOP_K = 4
SWIGLU_ALPHA = 1.702
SWIGLU_LIMIT = 7.0
EPS = 1e-6

LANES_V7X = 128
SUBLANES_V7X = 8
SUBLANES_BF16_V7X = 16
MXU_EDGE_V7X = 256
VMEM_LIMIT_V7X = 56 * 1024 * 1024

PROJ_ROWS = 256
ATTN_BLOCK = 256
ATTN_UNROLL = 4
SAMPLE_PAGE_GROUP = 8
ROUTE_ROWS = 256
MOE_ROWS = 256
MOE_SUPER_BLOCKS = 6
MOE_FF_CHUNK = 512
COMBINE_ROWS = 128
COMBINE_TILE = 256


def _cparams(sem, flags=None):
    return pltpu.CompilerParams(dimension_semantics=sem, vmem_limit_bytes=VMEM_LIMIT_V7X,
                                flags=flags)


def _rms(x, w):
    ms = jnp.mean(x * x, axis=-1, keepdims=True)
    return x * lax.rsqrt(ms + EPS) * w


def _dot(a, b):
    return jnp.dot(a, b, preferred_element_type=F32)


def _dot_nt(a, b):
    return lax.dot_general(a, b, (((1,), (1,)), ((), ())), preferred_element_type=F32)


LOG2E = 1.4426950408889634


def _sb_terms(z2):
    neg_abs = pltpu.bitcast(pltpu.bitcast(z2, jnp.uint32) | jnp.uint32(0x80000000), F32)
    w = jnp.exp2(neg_abs)
    sp = jnp.maximum(z2, 0.0) + jnp.log(1.0 + w) * LOG2E
    return sp, z2 - sp


def _qkv_kernel(x_ref, nw_ref, w_ref, qg_ref, kg_ref, gm_ref,
                qb_ref, kf_ref, vf_ref, kb_ref, vb_ref, *, width):
    xn = _rms(x_ref[...], nw_ref[...]).astype(BF16)
    p = _dot(xn, w_ref[...])
    gm = gm_ref[...]

    def head_norm(t, gain):
        sq = t * t
        hi = sq.astype(BF16)
        lo = (sq - hi.astype(F32)).astype(BF16)
        cols = []
        edge = gm.shape[0]
        for c in range(width // edge):
            sl = slice(c * edge, (c + 1) * edge)
            cols.append(_dot(hi[:, sl], gm) + _dot(lo[:, sl], gm))
        ms = jnp.concatenate(cols, axis=1)
        return t * lax.rsqrt(ms + EPS) * gain

    q = head_norm(p[:, :width], qg_ref[...]) * (HEAD_DIM ** -0.5 * LOG2E)
    k = head_norm(p[:, width:2 * width], kg_ref[...])
    v = p[:, 2 * width:]
    qb_ref[...] = q.astype(BF16)
    kb_ref[...] = k.astype(BF16)
    vb_ref[...] = v.astype(BF16)
    if len(kf_ref.shape) == 3:
        kf_ref[0] = k.T
        vf_ref[0] = v.T
    else:
        kf_ref[...] = k
        vf_ref[...] = v


def _qkv_proj(x, norm_w, w_qkv, q_gain, k_gain, head_mean, *, tm, seq=None):
    rows, d = x.shape
    width = w_qkv.shape[1] // 3
    const = lambda i: (0, 0)
    row = lambda i: (i, 0)
    out_b = jax.ShapeDtypeStruct((rows, width), BF16)
    if seq is None:
        out_f = jax.ShapeDtypeStruct((rows, width), F32)
        spec_f = pl.BlockSpec((tm, width), row)
    else:
        tiles = seq // tm
        out_f = jax.ShapeDtypeStruct((rows // seq, width, seq), F32)
        spec_f = pl.BlockSpec((1, width, tm), lambda i: (i // tiles, 0, i % tiles))
    return pl.pallas_call(
        functools.partial(_qkv_kernel, width=width),
        out_shape=(out_b, out_f, out_f, out_b, out_b),
        grid=(rows // tm,),
        in_specs=[pl.BlockSpec((tm, d), row),
                  pl.BlockSpec((1, d), const),
                  pl.BlockSpec(w_qkv.shape, const),
                  pl.BlockSpec((1, width), const),
                  pl.BlockSpec((1, width), const),
                  pl.BlockSpec(head_mean.shape, const)],
        out_specs=(pl.BlockSpec((tm, width), row), spec_f, spec_f,
                   pl.BlockSpec((tm, width), row), pl.BlockSpec((tm, width), row)),
        compiler_params=_cparams(("arbitrary",)),
        name="qkv_proj",
    )(x, norm_w, w_qkv, q_gain, k_gain, head_mean)


def _conv_body(x_ref, nw_ref, w_ref, cw_ref, og_ref, ch):
    xn = _rms(x_ref[...], nw_ref[...]).astype(BF16)
    p = _dot(xn, w_ref[...])
    return p[:, :ch], p[:, ch:2 * ch] * p[:, 2 * ch:]


def _conv_finish(bg, u, u1, u2, cw_ref, og_ref, out_ref):
    cw = cw_ref[...]
    y = bg * (cw[0:1, :] * u2 + cw[1:2, :] * u1 + cw[2:3, :] * u)
    out_ref[...] = _rms(y, og_ref[...]).astype(BF16)


def _conv_prompt_kernel(x_ref, nw_ref, w_ref, cw_ref, og_ref, out_ref, tail_ref, carry_ref,
                        *, ch, tiles_per_seq):
    bg, u = _conv_body(x_ref, nw_ref, w_ref, cw_ref, og_ref, ch)
    tm = u.shape[0]
    first = (pl.program_id(0) % tiles_per_seq) == 0
    prev = jnp.where(first, 0.0, carry_ref[...])
    c1, c2 = prev[7:8, :], prev[6:7, :]
    r = lax.broadcasted_iota(jnp.int32, u.shape, 0)
    u1 = jnp.where(r == 0, c1, pltpu.roll(u, 1, 0))
    u2 = jnp.where(r == 0, c2, jnp.where(r == 1, c1, pltpu.roll(u, 2, 0)))
    tail = u[tm - 8:, :]
    carry_ref[...] = tail
    tail_ref[0] = tail
    _conv_finish(bg, u, u1, u2, cw_ref, og_ref, out_ref)


def _conv_sample_kernel(x_ref, nw_ref, w_ref, cw_ref, og_ref, f1_ref, f2_ref, out_ref, u_ref,
                        *, ch, seq):
    bg, u = _conv_body(x_ref, nw_ref, w_ref, cw_ref, og_ref, ch)
    pos = lax.rem(lax.broadcasted_iota(jnp.int32, u.shape, 0), seq)
    u1 = jnp.where(pos == 0, f1_ref[...], pltpu.roll(u, 1, 0))
    u2 = jnp.where(pos < 2, f2_ref[...], pltpu.roll(u, 2, 0))
    u_ref[...] = u
    _conv_finish(bg, u, u1, u2, cw_ref, og_ref, out_ref)


def _conv_prompt(x, norm_w, w_conv, conv_w, out_gain, *, tm, seq):
    rows, d = x.shape
    ch = w_conv.shape[1] // 3
    n_tiles = rows // tm
    const = lambda i: (0, 0)
    row = lambda i: (i, 0)
    return pl.pallas_call(
        functools.partial(_conv_prompt_kernel, ch=ch, tiles_per_seq=seq // tm),
        out_shape=(jax.ShapeDtypeStruct((rows, ch), BF16),
                   jax.ShapeDtypeStruct((n_tiles, 8, ch), F32)),
        grid=(n_tiles,),
        in_specs=[pl.BlockSpec((tm, d), row),
                  pl.BlockSpec((1, d), const),
                  pl.BlockSpec(w_conv.shape, const),
                  pl.BlockSpec(conv_w.shape, const),
                  pl.BlockSpec((1, ch), const)],
        out_specs=(pl.BlockSpec((tm, ch), row),
                   pl.BlockSpec((1, 8, ch), lambda i: (i, 0, 0))),
        scratch_shapes=[pltpu.VMEM((8, ch), F32)],
        compiler_params=_cparams(("arbitrary",)),
        name="conv_prompt",
    )(x, norm_w, w_conv, conv_w, out_gain)


def _conv_sample(x, norm_w, w_conv, conv_w, out_gain, fix1, fix2, *, seq):
    rows, d = x.shape
    ch = w_conv.shape[1] // 3
    full = lambda a: pl.BlockSpec(a.shape, lambda i: (0, 0))
    return pl.pallas_call(
        functools.partial(_conv_sample_kernel, ch=ch, seq=seq),
        out_shape=(jax.ShapeDtypeStruct((rows, ch), BF16),
                   jax.ShapeDtypeStruct((rows, ch), F32)),
        grid=(1,),
        in_specs=[full(x), full(norm_w), full(w_conv), full(conv_w), full(out_gain),
                  full(fix1), full(fix2)],
        out_specs=(pl.BlockSpec((rows, ch), lambda i: (0, 0)),
                   pl.BlockSpec((rows, ch), lambda i: (0, 0))),
        compiler_params=_cparams(("arbitrary",)),
        name="conv_sample",
    )(x, norm_w, w_conv, conv_w, out_gain, fix1, fix2)


def _attn_prompt_kernel(bias_ref, q_ref, k_ref, v_ref, uo_ref, o_ref, *, blk):
    hp = pl.program_id(1)
    qi = pl.program_id(2)
    q2 = q_ref[0]
    lane = lax.broadcasted_iota(jnp.int32, (1, 2 * HEAD_DIM), 1)
    head_lanes = (lane < HEAD_DIM, lane >= HEAD_DIM)
    zero = jnp.zeros((), BF16)
    qm = tuple(jnp.where(m, q2, zero) for m in head_lanes)
    bias = (bias_ref[2 * hp], bias_ref[2 * hp + 1])
    uo = uo_ref[...]

    def block(kb, carries, acc, valid):
        start = pl.multiple_of(kb * blk, blk)
        k2 = k_ref[0, pl.ds(start, blk), :]
        v2 = v_ref[0, pl.ds(start, blk), :]
        new_carries = []
        for h in range(2):
            z2 = _dot_nt(qm[h], k2) + bias[h]
            sp, lsig = _sb_terms(z2)
            if valid is not None:
                sp = jnp.where(valid, sp, 0.0)
            later = _dot(sp.astype(BF16), uo)
            a = jnp.exp2(lsig - later - carries[h])
            if valid is not None:
                a = jnp.where(valid, a, 0.0)
            vm = jnp.where(head_lanes[h], v2, zero)
            acc = acc + _dot(a.astype(BF16), vm)
            new_carries.append(carries[h] + jnp.sum(sp, axis=1, keepdims=True))
        return tuple(new_carries), acc

    t_idx = lax.broadcasted_iota(jnp.int32, (blk, blk), 0)
    s_idx = lax.broadcasted_iota(jnp.int32, (blk, blk), 1)
    zero_col = jnp.zeros((blk, 1), F32)
    state = block(qi, (zero_col, zero_col), jnp.zeros((blk, 2 * HEAD_DIM), F32), s_idx < t_idx)

    n_single = qi & (ATTN_UNROLL - 1)

    def single(it, st):
        return block(qi - 1 - it, st[0], st[1], None)

    def group(it, st):
        kb = qi - 1 - n_single - ATTN_UNROLL * it
        for j in range(ATTN_UNROLL):
            st = block(kb - j, st[0], st[1], None)
        return st

    state = lax.fori_loop(0, n_single, single, state)
    state = lax.fori_loop(0, lax.shift_right_logical(qi, ATTN_UNROLL.bit_length() - 1), group,
                          state)
    o_ref[0] = state[1]


def _attn_prompt(qb, kb, vb, bias2, uo, *, blk):
    b, t, width = qb.shape
    n_pairs = width // (2 * HEAD_DIM)
    grid_spec = pltpu.PrefetchScalarGridSpec(
        num_scalar_prefetch=1,
        grid=(b, n_pairs, t // blk),
        in_specs=[pl.BlockSpec((1, blk, 2 * HEAD_DIM), lambda bi, hp, qi, s: (bi, qi, hp)),
                  pl.BlockSpec((1, t, 2 * HEAD_DIM), lambda bi, hp, qi, s: (bi, 0, hp)),
                  pl.BlockSpec((1, t, 2 * HEAD_DIM), lambda bi, hp, qi, s: (bi, 0, hp)),
                  pl.BlockSpec(uo.shape, lambda bi, hp, qi, s: (0, 0))],
        out_specs=pl.BlockSpec((1, blk, 2 * HEAD_DIM), lambda bi, hp, qi, s: (bi, qi, hp)),
    )
    return pl.pallas_call(
        functools.partial(_attn_prompt_kernel, blk=blk),
        out_shape=jax.ShapeDtypeStruct((b, t, width), F32),
        grid_spec=grid_spec,
        compiler_params=_cparams(("arbitrary", "arbitrary", "arbitrary")),
        name="attn_prompt",
    )(bias2, qb, kb, vb, uo)


def _attn_sample_kernel(pt_ref, qbd_ref, bias_ref, kn_ref, vn_ref, lo_new_ref, lo_ref,
                        kc_hbm, vc_hbm, o_ref, kbuf, vbuf, carry_ref, sem,
                        *, n_new, n_pages, group):
    b = pl.program_id(0)
    s = pl.program_id(1)
    n_steps = pl.num_programs(1)
    step = b * n_steps + s
    total = pl.num_programs(0) * n_steps

    def copies(bb, ss, slot):
        out = []
        for g in range(group):
            pg = pt_ref[bb * n_pages + n_pages - (ss + 1) * group + g]
            out.append(pltpu.make_async_copy(kc_hbm.at[pg], kbuf.at[slot, g], sem.at[slot]))
            out.append(pltpu.make_async_copy(vc_hbm.at[pg], vbuf.at[slot, g], sem.at[slot]))
        return out

    @pl.when(step == 0)
    def _first_fetch():
        for c in copies(b, s, 0):
            c.start()

    @pl.when(step + 1 < total)
    def _next_fetch():
        same = s + 1 < n_steps
        for c in copies(jnp.where(same, b, b + 1), jnp.where(same, s + 1, 0), (step + 1) % 2):
            c.start()

    slot = step % 2
    for c in copies(b, s, slot):
        c.wait()

    qbd = qbd_ref[0]
    bias = bias_ref[...]

    def weights(z2, lower, carry, valid):
        sp, lsig = _sb_terms(z2)
        if valid is not None:
            sp = jnp.where(valid, sp, 0.0)
        later = _dot(sp.astype(BF16), lower)
        a = jnp.exp2(lsig - later - carry)
        if valid is not None:
            a = jnp.where(valid, a, 0.0)
        return a.astype(BF16), carry + jnp.sum(sp, axis=1, keepdims=True)

    @pl.when(s == 0)
    def _new_tokens():
        kn = kn_ref[0].astype(BF16)
        z2 = _dot_nt(qbd, kn) + bias
        i = lax.rem(lax.broadcasted_iota(jnp.int32, z2.shape, 0), n_new)
        j = lax.broadcasted_iota(jnp.int32, z2.shape, 1)
        a, c = weights(z2, lo_new_ref[...], jnp.zeros_like(bias), j < i)
        o_ref[0] = _dot(a, vn_ref[0].astype(BF16))
        carry_ref[...] = c

    k_t = jnp.concatenate([kbuf[slot, g].astype(BF16) for g in range(group)], axis=1)
    v_t = jnp.concatenate([vbuf[slot, g].astype(BF16) for g in range(group)], axis=1)
    a, c = weights(_dot(qbd, k_t) + bias, lo_ref[...], carry_ref[...], None)
    o_ref[0] += _dot_nt(a, v_t)
    carry_ref[...] = c


def _attn_sample(page_table, qbd, bias_rows, k_new, v_new, cache_kt, cache_vt, lower_new, lower,
                 *, n_new):
    n_seq, n_pages = page_table.shape
    _, width, page = cache_kt.shape
    rows = qbd.shape[1]
    group = lower.shape[0] // page
    assert n_pages % group == 0
    per_seq = lambda b, s, pt: (b, 0, 0)
    const = lambda b, s, pt: (0, 0)
    any_spec = pl.BlockSpec(memory_space=pl.ANY)
    grid_spec = pltpu.PrefetchScalarGridSpec(
        num_scalar_prefetch=1,
        grid=(n_seq, n_pages // group),
        in_specs=[pl.BlockSpec((1, rows, width), per_seq),
                  pl.BlockSpec((rows, 1), const),
                  pl.BlockSpec((1,) + k_new.shape[1:], per_seq),
                  pl.BlockSpec((1,) + v_new.shape[1:], per_seq),
                  pl.BlockSpec(lower_new.shape, const),
                  pl.BlockSpec(lower.shape, const),
                  any_spec, any_spec],
        out_specs=pl.BlockSpec((1, rows, width), per_seq),
        scratch_shapes=[pltpu.VMEM((2, group, width, page), F32),
                        pltpu.VMEM((2, group, width, page), F32),
                        pltpu.VMEM((rows, 1), F32),
                        pltpu.SemaphoreType.DMA((2,))],
    )
    return pl.pallas_call(
        functools.partial(_attn_sample_kernel, n_new=n_new, n_pages=n_pages, group=group),
        out_shape=jax.ShapeDtypeStruct((n_seq, rows, width), F32),
        grid_spec=grid_spec,
        compiler_params=_cparams(("arbitrary", "arbitrary")),
        name="attn_sample",
    )(page_table.reshape(-1), qbd, bias_rows, k_new, v_new, lower_new, lower, cache_kt, cache_vt)


def _route_kernel(attn_ref, conv_ref, x_ref, ag_ref, wo_ref, fg_ref, wr_ref, br_ref, cin_ref,
                  ut_ref, ones_ref, *rest, n_valid, tile0, n_exp):
    h_ref, xp_ref, topi_ref, gate_ref, rank_ref, cnt_ref, cnt_scr = rest[-7:]
    i = pl.program_id(0)
    tm = x_ref.shape[0]
    half = attn_ref.shape[1]

    @pl.when(i == 0)
    def _():
        cnt_scr[...] = cin_ref[...]

    an = _rms(attn_ref[...], ag_ref[...]).astype(BF16)
    mix = _dot(an, wo_ref[:half, :]) + _dot(conv_ref[...], wo_ref[half:, :])
    h = x_ref[...] + mix
    h_ref[...] = h

    xn = _rms(h, fg_ref[...])
    hi = xn.astype(BF16)
    hi32 = hi.astype(F32)
    lo = (xn - hi32).astype(BF16)
    d2 = xn.shape[1] // 2
    bits = pltpu.bitcast(hi32, jnp.uint32)
    xp_ref[...] = (bits[:, :d2] & jnp.uint32(0xFFFF0000)) | (bits[:, d2:] >> 16)

    wr = wr_ref[...]
    l1 = _dot_nt(wr, hi)
    l2 = _dot_nt(wr[:n_exp, :], lo)
    logits = l1[:n_exp, :] + l1[n_exp:, :] + l2 + br_ref[...]

    e_idx = lax.broadcasted_iota(jnp.int32, logits.shape, 0)
    rest = logits
    tops, idxs, hots = [], [], []
    for _ in range(TOP_K):
        m = jnp.max(rest, axis=0, keepdims=True)
        idx = jnp.min(jnp.where(rest == m, e_idx, n_exp), axis=0, keepdims=True)
        hot = e_idx == idx
        rest = jnp.where(hot, -jnp.inf, rest)
        tops.append(m)
        idxs.append(idx)
        hots.append(hot)
    ex = [jnp.exp(m - tops[0]) for m in tops]
    denom = ex[0] + ex[1] + ex[2] + ex[3]
    gates = [e / denom for e in ex]

    tok = (tile0 + i) * tm + lax.broadcasted_iota(jnp.int32, (1, tm), 1)
    valid = tok < n_valid
    chosen = jnp.where((hots[0] | hots[1] | hots[2] | hots[3]) & valid, 1.0, 0.0).astype(BF16)
    before = _dot(chosen, ut_ref[...])
    cnt = cnt_scr[...]
    base = jnp.concatenate([cnt] * (tm // LANES_V7X), axis=1) + before
    ranks = [jnp.sum(jnp.where(hot, base, 0.0), axis=0, keepdims=True) for hot in hots]
    cnt = cnt + _dot(chosen, ones_ref[...])
    cnt_scr[...] = cnt
    cnt_ref[...] = cnt

    topi_ref[...] = jnp.concatenate(idxs, axis=0)
    gate_ref[...] = jnp.concatenate(gates, axis=0)
    rank_ref[...] = jnp.concatenate(ranks, axis=0).astype(jnp.int32)


def _route(attn, convn, x, attn_gain, w_out, ffn_gain, wr_t, br_cols, cnt_in, ut, ones,
           prev, *, tm, tile0, n_valid, rows_total):
    rows, d = x.shape
    half = attn.shape[1]
    n_exp = br_cols.shape[0]
    n_tiles = rows // tm
    const = lambda i: (0, 0)
    row = lambda i: (i, 0)
    orow = lambda i: (tile0 + i, 0)
    ocol = lambda i: (0, tile0 + i)
    any_spec = pl.BlockSpec(memory_space=pl.ANY)
    out_shape = (jax.ShapeDtypeStruct((rows_total, d), F32),
                 jax.ShapeDtypeStruct((rows_total, d // 2), jnp.uint32),
                 jax.ShapeDtypeStruct((TOP_K, rows_total), jnp.int32),
                 jax.ShapeDtypeStruct((TOP_K, rows_total), F32),
                 jax.ShapeDtypeStruct((TOP_K, rows_total), jnp.int32),
                 jax.ShapeDtypeStruct((n_exp, LANES_V7X), F32))
    prev = () if prev is None else tuple(prev)
    aliases = {11 + j: j for j in range(len(prev))}
    return pl.pallas_call(
        functools.partial(_route_kernel, n_valid=n_valid, tile0=tile0, n_exp=n_exp),
        out_shape=out_shape,
        grid=(n_tiles,),
        in_specs=[pl.BlockSpec((tm, half), row),
                  pl.BlockSpec((tm, half), row),
                  pl.BlockSpec((tm, d), row),
                  pl.BlockSpec((1, half), const),
                  pl.BlockSpec(w_out.shape, const),
                  pl.BlockSpec((1, d), const),
                  pl.BlockSpec(wr_t.shape, const),
                  pl.BlockSpec(br_cols.shape, const),
                  pl.BlockSpec(cnt_in.shape, const),
                  pl.BlockSpec(ut.shape, const),
                  pl.BlockSpec(ones.shape, const)] + [any_spec] * len(prev),
        out_specs=(pl.BlockSpec((tm, d), orow),
                   pl.BlockSpec((tm, d // 2), orow),
                   pl.BlockSpec((TOP_K, tm), ocol),
                   pl.BlockSpec((TOP_K, tm), ocol),
                   pl.BlockSpec((TOP_K, tm), ocol),
                   pl.BlockSpec((n_exp, LANES_V7X), const)),
        scratch_shapes=[pltpu.VMEM((n_exp, LANES_V7X), F32)],
        input_output_aliases=aliases,
        compiler_params=_cparams(("arbitrary",)),
        name="route",
    )(attn, convn, x, attn_gain, w_out, ffn_gain, wr_t, br_cols, cnt_in, ut, ones, *prev)


def _moe_kernel(sbe_ref, sb0_ref, sbn_ref, tok_hbm, x_hbm, w1g_ref, w1l_ref, b1g_ref, b1l_ref,
                w2_ref, b2_ref, y_hbm, xbuf, yacc, tok_smem, sem_tok, sem_x, sem_y, *, tmb):
    g = pl.program_id(0)
    fc = pl.program_id(1)
    n_fc = pl.num_programs(1)
    nb = sbn_ref[g]
    blk0 = sb0_ref[g]
    d2 = xbuf.shape[1]
    d = yacc.shape[1]

    @pl.when(nb > 0)
    def _active():
        @pl.when(fc == 0)
        def _gather():
            cp = pltpu.make_async_copy(tok_hbm.at[pl.ds(blk0, tok_smem.shape[0])], tok_smem,
                                       sem_tok)
            cp.start()
            cp.wait()

            def per_block(rb, carry):
                base = pl.multiple_of(rb * tmb, tmb)
                for part in range(tmb // LANES_V7X):

                    def per_tile(i, c, part=part):
                        r0 = pl.multiple_of(i * SUBLANES_V7X, SUBLANES_V7X)
                        for u in range(SUBLANES_V7X):
                            t = tok_smem[rb, part, r0 + u]
                            pltpu.make_async_copy(
                                x_hbm.at[pl.ds(t, 1), :],
                                xbuf.at[pl.ds(base + part * LANES_V7X + r0 + u, 1), :],
                                sem_x).start()
                        return c

                    lax.fori_loop(0, LANES_V7X // SUBLANES_V7X, per_tile, 0)
                yacc[pl.ds(base, tmb), :] = jnp.broadcast_to(b2_ref[0], (tmb, d))
                return carry

            def wait_block(rb, carry):
                pltpu.make_async_copy(x_hbm.at[pl.ds(0, tmb), :], xbuf.at[pl.ds(0, tmb), :],
                                      sem_x).wait()
                return carry

            lax.fori_loop(0, nb, per_block, 0)
            lax.fori_loop(0, nb, wait_block, 0)

        def per_block(rb, carry):
            base = pl.multiple_of(rb * tmb, tmb)
            xp = xbuf[pl.ds(base, tmb), :]
            xa = pltpu.bitcast(xp & jnp.uint32(0xFFFF0000), F32).astype(BF16)
            xb = pltpu.bitcast(xp << 16, F32).astype(BF16)
            hg = _dot(xa, w1g_ref[0, :d2, :]) + _dot(xb, w1g_ref[0, d2:, :]) + b1g_ref[0]
            hl = _dot(xa, w1l_ref[0, :d2, :]) + _dot(xb, w1l_ref[0, d2:, :]) + b1l_ref[0]
            hg = jnp.minimum(hg, SWIGLU_LIMIT)
            hl = jnp.clip(hl, -SWIGLU_LIMIT, SWIGLU_LIMIT)
            act = hg * jax.nn.sigmoid(SWIGLU_ALPHA * hg) * (hl + 1.0)
            yacc[pl.ds(base, tmb), :] += _dot(act.astype(BF16), w2_ref[0])
            return carry

        lax.fori_loop(0, nb, per_block, 0)

        @pl.when(fc == n_fc - 1)
        def _store():
            def start(rb, carry):
                base = pl.multiple_of(rb * tmb, tmb)
                dst = pl.multiple_of((blk0 + rb) * tmb, tmb)
                pltpu.make_async_copy(yacc.at[pl.ds(base, tmb), :],
                                      y_hbm.at[pl.ds(dst, tmb), :], sem_y).start()
                return carry

            def wait(rb, carry):
                pltpu.make_async_copy(yacc.at[pl.ds(0, tmb), :],
                                      y_hbm.at[pl.ds(0, tmb), :], sem_y).wait()
                return carry

            lax.fori_loop(0, nb, start, 0)
            lax.fori_loop(0, nb, wait, 0)


def _moe(sb_expert, sb_block0, sb_nblocks, row_tok, xp_all, w1, b1, w2, b2, *, n_blocks):
    n_exp, d, f2 = w1.shape
    f = f2 // 2
    tmb = row_tok.shape[1] * row_tok.shape[2]
    fcw = MOE_FF_CHUNK
    n_fc = f // fcw
    n_super = sb_expert.shape[0]
    sb_rows = MOE_SUPER_BLOCKS * tmb
    any_spec = pl.BlockSpec(memory_space=pl.ANY)
    grid_spec = pltpu.PrefetchScalarGridSpec(
        num_scalar_prefetch=3,
        grid=(n_super, n_fc),
        in_specs=[any_spec, any_spec,
                  pl.BlockSpec((1, d, fcw), lambda g, c, e, b0, nb: (e[g], 0, c)),
                  pl.BlockSpec((1, d, fcw), lambda g, c, e, b0, nb: (e[g], 0, n_fc + c)),
                  pl.BlockSpec((1, 1, fcw), lambda g, c, e, b0, nb: (e[g], 0, c)),
                  pl.BlockSpec((1, 1, fcw), lambda g, c, e, b0, nb: (e[g], 0, n_fc + c)),
                  pl.BlockSpec((1, fcw, d), lambda g, c, e, b0, nb: (e[g], c, 0)),
                  pl.BlockSpec((1, 1, d), lambda g, c, e, b0, nb: (e[g], 0, 0))],
        out_specs=any_spec,
        scratch_shapes=[pltpu.VMEM((sb_rows, d // 2), jnp.uint32),
                        pltpu.VMEM((sb_rows, d), F32),
                        pltpu.SMEM((MOE_SUPER_BLOCKS,) + row_tok.shape[1:], jnp.int32),
                        pltpu.SemaphoreType.DMA,
                        pltpu.SemaphoreType.DMA,
                        pltpu.SemaphoreType.DMA],
    )
    return pl.pallas_call(
        functools.partial(_moe_kernel, tmb=tmb),
        out_shape=jax.ShapeDtypeStruct((n_blocks * tmb, d), F32),
        grid_spec=grid_spec,
        compiler_params=_cparams(("arbitrary", "arbitrary")),
        name="moe_experts",
    )(sb_expert, sb_block0, sb_nblocks, row_tok, xp_all, w1, w1,
      b1.reshape(n_exp, 1, f2), b1.reshape(n_exp, 1, f2), w2, b2.reshape(n_exp, 1, d))


def _combine_kernel(dest_hbm, y_hbm, h_ref, gate_ref, o_ref, ybuf, dest_smem, sem_d, sem_y,
                    *, tile0):
    i = pl.program_id(0)
    tm = h_ref.shape[0]
    cp = pltpu.make_async_copy(dest_hbm.at[tile0 + i], dest_smem, sem_d)
    cp.start()
    cp.wait()

    def per_tile(i, c):
        j0 = pl.multiple_of(i * SUBLANES_V7X, SUBLANES_V7X)
        for u in range(SUBLANES_V7X):
            r = dest_smem[j0 + u]
            pltpu.make_async_copy(y_hbm.at[pl.ds(r, 1), :], ybuf.at[pl.ds(j0 + u, 1), :],
                                  sem_y).start()
        return c

    lax.fori_loop(0, TOP_K * tm // SUBLANES_V7X, per_tile, 0)
    pltpu.make_async_copy(y_hbm.at[pl.ds(0, TOP_K * tm), :], ybuf, sem_y).wait()
    out = h_ref[...]
    gates = gate_ref[...]
    for k in range(TOP_K):
        out = out + gates[:, k:k + 1] * ybuf[k * tm:(k + 1) * tm, :]
    o_ref[...] = out


def _combine(dest, y_rows, h_all, gates_t, *, tm, tile0, n_tiles):
    d = h_all.shape[1]
    any_spec = pl.BlockSpec(memory_space=pl.ANY)
    return pl.pallas_call(
        functools.partial(_combine_kernel, tile0=tile0),
        out_shape=jax.ShapeDtypeStruct((n_tiles * tm, d), F32),
        grid=(n_tiles,),
        in_specs=[any_spec, any_spec,
                  pl.BlockSpec((tm, d), lambda i: (tile0 + i, 0)),
                  pl.BlockSpec((tm, TOP_K), lambda i: (tile0 + i, 0))],
        out_specs=pl.BlockSpec((tm, d), lambda i: (i, 0)),
        scratch_shapes=[pltpu.VMEM((TOP_K * tm, d), F32),
                        pltpu.SMEM((TOP_K * tm,), jnp.int32),
                        pltpu.SemaphoreType.DMA,
                        pltpu.SemaphoreType.DMA],
        compiler_params=_cparams(("arbitrary",)),
        name="moe_combine",
    )(dest, y_rows, h_all, gates_t)


def _strict_upper(n, dtype=BF16):
    r = lax.broadcasted_iota(jnp.int32, (n, n), 0)
    c = lax.broadcasted_iota(jnp.int32, (n, n), 1)
    return (r < c).astype(dtype)


def _lower_incl(n, dtype=BF16):
    r = lax.broadcasted_iota(jnp.int32, (n, n), 0)
    c = lax.broadcasted_iota(jnp.int32, (n, n), 1)
    return (r >= c).astype(dtype)


def _layer(xp, xs, cache_k, cache_v, state_conv, page_table, attn_norm_w, w_in, q_norm_w,
           k_norm_w, sb_bias, conv_w, attn_out_norm_w, conv_out_norm_w, w_out, ffn_norm_w,
           w_router, b_router, w1, b1, w2, b2):
    n_b, seq, d = xp.shape
    n_dec, n_new, _ = xs.shape
    width = w_in.shape[1] // 6
    n_heads = width // HEAD_DIM
    n_exp = w_router.shape[1]
    n_prompt = n_b * seq
    n_sample = n_dec * n_new
    assert n_new >= CONV_WIDTH - 1 and conv_w.shape[0] == CONV_WIDTH
    assert seq % ATTN_BLOCK == 0 and seq % PROJ_ROWS == 0 and n_prompt % ROUTE_ROWS == 0
    assert n_sample <= COMBINE_ROWS and n_prompt % COMBINE_ROWS == 0

    row = lambda a: a.reshape(1, -1)
    w_in_b = w_in.astype(BF16)
    w_qkv, w_conv = w_in_b[:, :3 * width], w_in_b[:, 3 * width:]
    w_out_b = w_out.astype(BF16)
    edge = min(MXU_EDGE_V7X, width)
    head_in_tile = (lax.broadcasted_iota(jnp.int32, (edge, edge), 0) // HEAD_DIM ==
                    lax.broadcasted_iota(jnp.int32, (edge, edge), 1) // HEAD_DIM)
    head_mean = jnp.where(head_in_tile, 1.0 / HEAD_DIM, 0.0).astype(BF16)
    q_gain = row(jnp.tile(q_norm_w, n_heads))
    k_gain = row(jnp.tile(k_norm_w, n_heads))

    xp2 = xp.reshape(n_prompt, d)
    qb, kf_t, vf_t, kb, vb = _qkv_proj(xp2, row(attn_norm_w), w_qkv, q_gain, k_gain, head_mean,
                                       tm=PROJ_ROWS, seq=seq)
    convn_p, tails = _conv_prompt(xp2, row(attn_norm_w), w_conv, conv_w, row(conv_out_norm_w),
                                  tm=PROJ_ROWS, seq=seq)
    tiles_per_seq = seq // PROJ_ROWS
    conv_state_p = tails[tiles_per_seq - 1::tiles_per_seq, 8 - (CONV_WIDTH - 1):, :]
    shp = (n_b, seq, width)
    bias2 = sb_bias * LOG2E
    uo = _strict_upper(ATTN_BLOCK).T
    attn_p = _attn_prompt(qb.reshape(shp), kb.reshape(shp), vb.reshape(shp), bias2, uo,
                          blk=ATTN_BLOCK)

    xs2 = xs.reshape(n_sample, d)
    qsb, ksf, vsf, _, _ = _qkv_proj(xs2, row(attn_norm_w), w_qkv, q_gain, k_gain, head_mean,
                                    tm=n_sample)
    zeros_fix = jnp.zeros((n_dec, n_new, width), F32)
    fix1 = zeros_fix.at[:, 0].set(state_conv[:, 1]).reshape(n_sample, width)
    fix2 = zeros_fix.at[:, 0].set(state_conv[:, 0]).at[:, 1].set(state_conv[:, 1])
    convn_s, u_s = _conv_sample(xs2, row(attn_norm_w), w_conv, conv_w, row(conv_out_norm_w),
                                fix1, fix2.reshape(n_sample, width), seq=n_new)
    conv_state_s = u_s.reshape(n_dec, n_new, width)[:, n_new - (CONV_WIDTH - 1):, :]

    q4 = qsb.reshape(n_dec, n_new, n_heads, HEAD_DIM)
    eye = jnp.eye(n_heads, dtype=BF16)
    qbd = jnp.einsum("bihd,hg->bhigd", q4, eye).reshape(n_dec, n_heads * n_new, width)
    bias_rows = jnp.repeat(bias2, n_new)[:, None]
    pad_new = lambda a: jnp.pad(a.reshape(n_dec, n_new, width),
                                ((0, 0), (0, SUBLANES_BF16_V7X - n_new), (0, 0)))
    n_phys, page = cache_k.shape[0], cache_k.shape[1]
    group = math.gcd(page_table.shape[1], SAMPLE_PAGE_GROUP)
    pages_t = lambda c: jnp.transpose(c, (0, 2, 3, 1)).reshape(n_phys, width, page)
    acc_s = _attn_sample(page_table, qbd, bias_rows, pad_new(ksf), pad_new(vsf),
                         pages_t(cache_k), pages_t(cache_v),
                         _strict_upper(SUBLANES_BF16_V7X).T, _strict_upper(group * page).T,
                         n_new=n_new)
    acc_s = acc_s.reshape(n_dec, n_heads, n_new, n_heads, HEAD_DIM)
    attn_s = jnp.stack([acc_s[:, h, :, h, :] for h in range(n_heads)], axis=2)

    rows_total = n_prompt + COMBINE_ROWS
    n_tok = n_prompt + n_sample
    wr_hi = w_router.astype(BF16)
    wr_lo = (w_router - wr_hi.astype(F32)).astype(BF16)
    wr_t = jnp.concatenate([wr_hi.T, wr_lo.T], axis=0)
    ones = jnp.ones((ROUTE_ROWS, LANES_V7X), BF16)
    ut = _strict_upper(ROUTE_ROWS)
    br = lambda tm: jnp.broadcast_to(b_router[:, None], (n_exp, tm))
    cnt0 = jnp.zeros((n_exp, LANES_V7X), F32)
    outs = _route(attn_p.reshape(n_prompt, width), convn_p, xp2, row(attn_out_norm_w), w_out_b,
                  row(ffn_norm_w), wr_t, br(ROUTE_ROWS), cnt0, ut, ones, None,
                  tm=ROUTE_ROWS, tile0=0, n_valid=n_tok, rows_total=rows_total)
    pad_s = lambda a: jnp.pad(a, ((0, COMBINE_ROWS - n_sample), (0, 0)))
    outs = _route(pad_s(attn_s.reshape(n_sample, width)), pad_s(convn_s), pad_s(xs2),
                  row(attn_out_norm_w), w_out_b, row(ffn_norm_w), wr_t, br(COMBINE_ROWS),
                  outs[5], ut[:COMBINE_ROWS, :COMBINE_ROWS], ones[:COMBINE_ROWS], outs[:5],
                  tm=COMBINE_ROWS, tile0=n_prompt // COMBINE_ROWS, n_valid=n_tok,
                  rows_total=rows_total)
    h_all, xp_all, topi, gates, ranks, cnt = outs

    tmb = MOE_ROWS
    counts = cnt[:, 0].astype(jnp.int32)
    nblk = (counts + tmb - 1) // tmb
    blk_end = jnp.cumsum(nblk)
    blk_off = blk_end - nblk
    n_blocks = (n_tok * TOP_K + n_exp * (tmb - 1)) // tmb
    tok_ids = jnp.arange(rows_total, dtype=jnp.int32)
    rb = MOE_SUPER_BLOCKS
    own = topi[..., None] == jnp.arange(n_exp, dtype=jnp.int32)
    first_row = jnp.sum(jnp.where(own, blk_off * tmb, 0), axis=-1)
    dest = jnp.where(tok_ids[None, :] < n_tok, first_row + ranks, (n_blocks + rb) * tmb)
    row_tok = jnp.zeros(((n_blocks + rb) * tmb,), jnp.int32).at[dest.reshape(-1)].set(
        jnp.tile(tok_ids, TOP_K), mode="drop").reshape(n_blocks + rb, tmb // LANES_V7X, LANES_V7X)
    dest = jnp.where(tok_ids[None, :] < n_tok, dest, 0)

    nsb = (nblk + rb - 1) // rb
    sb_end = jnp.cumsum(nsb)
    n_super = n_exp + n_blocks // rb
    gidx = jnp.arange(n_super, dtype=jnp.int32)
    last = jnp.maximum(sb_end[-1] - 1, 0)
    ge = jnp.minimum(gidx, last)
    sb_expert = jnp.searchsorted(sb_end, ge, side="right").astype(jnp.int32)
    local = ge - (sb_end - nsb)[sb_expert]
    sb_block0 = (blk_off[sb_expert] + local * rb).astype(jnp.int32)
    sb_nblocks = jnp.where(gidx < sb_end[-1],
                           jnp.clip(nblk[sb_expert] - local * rb, 0, rb), 0).astype(jnp.int32)

    y_rows = _moe(sb_expert, sb_block0, sb_nblocks, row_tok, xp_all, w1, b1, w2, b2,
                  n_blocks=n_blocks)

    def dest_tiles(dst, tm):
        n_tiles = dst.shape[1] // tm
        return dst.reshape(TOP_K, n_tiles, tm).transpose(1, 0, 2).reshape(n_tiles, TOP_K * tm)

    gates_t = gates.T
    yp = _combine(dest_tiles(dest[:, :n_prompt], COMBINE_TILE), y_rows, h_all, gates_t,
                  tm=COMBINE_TILE, tile0=0, n_tiles=n_prompt // COMBINE_TILE)
    ys = _combine(dest_tiles(dest, COMBINE_ROWS), y_rows, h_all, gates_t, tm=COMBINE_ROWS,
                  tile0=n_prompt // COMBINE_ROWS, n_tiles=1)

    heads_last = lambda a: jnp.transpose(a.reshape(n_b, n_heads, HEAD_DIM, seq), (0, 3, 1, 2))
    new_shape_s = (n_dec, n_new, n_heads, HEAD_DIM)
    return (yp.reshape(n_b, seq, d), ys[:n_sample].reshape(n_dec, n_new, d),
            heads_last(kf_t), heads_last(vf_t), conv_state_p,
            ksf.reshape(new_shape_s), vsf.reshape(new_shape_s), conv_state_s)


def kernel(x_prompt, x_sample, cache_k, cache_v, state_conv, page_table, attn_norm_w, w_in,
           q_norm_w, k_norm_w, sb_bias, conv_w, attn_out_norm_w, conv_out_norm_w, w_out,
           ffn_norm_w, w_router, b_router, w1, b1, w2, b2):
    depth = w_in.shape[0]
    xp, xs = x_prompt, x_sample
    outs = [[] for _ in range(6)]
    for l in range(depth):
        res = _layer(xp, xs, cache_k[l], cache_v[l], state_conv[l], page_table, attn_norm_w[l],
                     w_in[l], q_norm_w[l], k_norm_w[l], sb_bias[l], conv_w[l],
                     attn_out_norm_w[l], conv_out_norm_w[l], w_out[l], ffn_norm_w[l],
                     w_router[l], b_router[l], w1[l], b1[l], w2[l], b2[l])
        xp, xs = res[0], res[1]
        for acc, val in zip(outs, res[2:]):
            acc.append(val)
    return (xp, xs) + tuple(jnp.stack(o) for o in outs)
```
